```python
import jax, jax.numpy as jnp
from jax import lax
import numpy as np

D_MODEL = 4096
BATCH = 4
SEQ = 2048
DEPTH = 1
DEC_BATCH = 128
DEC_SEQ = 1
PAST_LEN = 16384
PAGE_SIZE = 128

D_A = D_MODEL // 2
D_B = D_MODEL // 2
HEAD_K = 128
HEAD_V = 128
H_A = D_A // HEAD_K
CHUNK = 64
CONV_W = 3
H_P = 8
D_QUERY = 256
D_HALF = D_QUERY // 2
N_KEYS = 128
N_EXP = N_KEYS * N_KEYS
TOP_K = 16
PEER_BLOCK = 64
EPS = 1e-6
IN_SIZES = (D_A, D_A, D_A, D_A, D_B, D_B, D_B, D_MODEL, D_MODEL)
IN_COLS = 4 * D_A + 3 * D_B + 2 * D_MODEL

kernel_name = "hgrn2_shortconv_peer_hybrid_step"


def rmsnorm(x, g):
    xf = x.astype(jnp.float32)
    y = xf * lax.rsqrt(jnp.mean(xf * xf, axis=-1, keepdims=True) + EPS)
    return (y * g.astype(jnp.float32)).astype(x.dtype)


def split_cols(a, sizes):
    idx = np.cumsum(np.array(sizes))[:-1].tolist()
    return jnp.split(a, idx, axis=-1)


def _chunks(a):
    b, t = a.shape[:2]
    return jnp.moveaxis(a.reshape((b, t // CHUNK, CHUNK) + a.shape[2:]), 1, 0)


def hgrn2_chunked(q, logf, k, v, s0):
    b, t, h, _ = q.shape
    mask = jnp.tril(jnp.ones((CHUNK, CHUNK), dtype=bool))[None, :, :, None, None]

    def step(s, inp):
        qc, gc, kc, vc = inp
        cum = jnp.cumsum(gc, axis=1)
        decay = jnp.exp(jnp.where(mask, cum[:, :, None] - cum[:, None, :], -jnp.inf))
        att = jnp.einsum('bthk,btshk,bshk->bhts', qc, decay, kc)
        o = (jnp.einsum('bhts,bshv->bthv', att, vc)
             + jnp.einsum('bthk,bhkv->bthv', qc * jnp.exp(cum), s))
        last = cum[:, -1]
        s_new = (jnp.exp(last)[..., None] * s
                 + jnp.einsum('bshk,bshv->bhkv', kc * jnp.exp(last[:, None] - cum), vc))
        return s_new, o

    s_fin, o = lax.scan(step, s0, (_chunks(q), _chunks(logf), _chunks(k), _chunks(v)))
    o = jnp.moveaxis(o, 0, 1).reshape(b, t, h, -1)
    return o, s_fin


def hgrn2_recurrent(q, logf, k, v, s0):
    def step(s, inp):
        qt, gt, kt, vt = inp
        s = jnp.exp(gt)[..., None] * s + kt[..., None] * vt[..., None, :]
        return s, jnp.einsum('bhk,bhkv->bhv', qt, s)
    xs = (jnp.moveaxis(q, 1, 0), jnp.moveaxis(logf, 1, 0), jnp.moveaxis(k, 1, 0), jnp.moveaxis(v, 1, 0))
    s_fin, o = lax.scan(step, s0, xs)
    return jnp.moveaxis(o, 0, 1), s_fin


def causal_conv(u, buf, w):
    t = u.shape[1]
    full = jnp.concatenate([buf.astype(u.dtype), u], axis=1)
    y = sum(full[:, j:j + t] * w[j] for j in range(CONV_W))
    return y, full[:, -(CONV_W - 1):]


def peer_ffn(x2d, wq, subkeys, u_tab, v_tab):
    t = x2d.shape[0]
    q = (x2d @ wq).reshape(t, H_P, 2, D_HALF).astype(jnp.float32)
    s = jnp.einsum('thpd,hpnd->thpn', q, subkeys.astype(jnp.float32))
    v1, i1 = lax.top_k(s[:, :, 0], TOP_K)
    v2, i2 = lax.top_k(s[:, :, 1], TOP_K)
    cand = (v1[..., :, None] + v2[..., None, :]).reshape(t, H_P, TOP_K * TOP_K)
    sc, pos = lax.top_k(cand, TOP_K)
    eid = (jnp.take_along_axis(i1, pos // TOP_K, axis=-1) * N_KEYS
           + jnp.take_along_axis(i2, pos % TOP_K, axis=-1))
    gate = jax.nn.softmax(sc, axis=-1).astype(x2d.dtype)
    nb = -(-t // PEER_BLOCK)
    pad = nb * PEER_BLOCK - t
    xb = jnp.pad(x2d, ((0, pad), (0, 0))).reshape(nb, PEER_BLOCK, D_MODEL)
    eb = jnp.pad(eid, ((0, pad), (0, 0), (0, 0))).reshape(nb, PEER_BLOCK, H_P, TOP_K)
    gb = jnp.pad(gate, ((0, pad), (0, 0), (0, 0))).reshape(nb, PEER_BLOCK, H_P, TOP_K)

    def block(args):
        xs, es, gs = args
        act = jax.nn.gelu(jnp.einsum('phkd,pd->phk', u_tab[es], xs))
        return jnp.einsum('phk,phkd->pd', gs * act, v_tab[es])

    y = lax.map(block, (xb, eb, gb)).reshape(nb * PEER_BLOCK, D_MODEL)
    return y[:t]


def hybrid_layer(x, c, s0, buf, lb, chunked, w_ada, b_ada, g_pre1, g_post1, g_pre2, g_post2,
                 w_in, onorm, conv_w, p_a, p_b, w_o, peer_wq, peer_subkeys, peer_u, peer_v):
    bsz, t, _ = x.shape
    mod = (jax.nn.silu(c) @ w_ada + b_ada)[:, None, :]
    sh1, sc1, gt1, sh2, sc2, gt2 = jnp.split(mod, 6, axis=-1)
    u = rmsnorm(x, g_pre1) * (1 + sc1) + sh1
    q, fz, iv, og, hb, bg, cg, ga, gb = split_cols(u @ w_in, IN_SIZES)
    hk = lambda a: a.reshape(bsz, t, H_A, HEAD_K).astype(jnp.float32)
    hv = lambda a: a.reshape(bsz, t, H_A, HEAD_V).astype(jnp.float32)
    f = lb + (1 - lb) * jax.nn.sigmoid(hk(fz))
    args = (hk(q), jnp.log(f), 1 - f, hv(iv))
    s0 = s0.astype(jnp.float32)
    if chunked:
        o, s_new = hgrn2_chunked(*args, s0)
    else:
        o, s_new = hgrn2_recurrent(*args, s0)
    o = rmsnorm(o, onorm).astype(x.dtype).reshape(bsz, t, D_A)
    y_a = (o * jax.nn.silu(og)) @ p_a
    conv_out, buf_new = causal_conv(cg * hb, buf, conv_w)
    y_b = (bg * conv_out) @ p_b
    merged = jax.nn.sigmoid(ga) * y_a + jax.nn.sigmoid(gb) * y_b
    x = x + gt1 * rmsnorm(merged @ w_o, g_post1)
    u2 = rmsnorm(x, g_pre2) * (1 + sc2) + sh2
    y2 = peer_ffn(u2.reshape(bsz * t, D_MODEL), peer_wq, peer_subkeys, peer_u, peer_v).reshape(bsz, t, D_MODEL)
    x = x + gt2 * rmsnorm(y2, g_post2)
    return x, s_new.astype(x.dtype), buf_new


def setup_inputs(seed: int = 0) -> dict:
    key = jax.random.key(seed)
    ks = jax.random.split(key, 24)
    nrm = lambda k, shape, s: jax.random.normal(k, shape, jnp.float32) * s
    D = D_MODEL
    return {
        "x_prompt": nrm(ks[0], (BATCH, SEQ, D), 1.0),
        "x_sample": nrm(ks[1], (DEC_BATCH, DEC_SEQ, D), 1.0),
        "state_hgrn": nrm(ks[2], (DEPTH, DEC_BATCH, H_A, HEAD_K, HEAD_V), 0.5),
        "state_conv": nrm(ks[3], (DEPTH, DEC_BATCH, CONV_W - 1, D_B), 1.0),
        "c_prompt": nrm(ks[4], (BATCH, D), 1.0),
        "c_sample": nrm(ks[5], (DEC_BATCH, D), 1.0),
        "w_ada": nrm(ks[6], (DEPTH, D, 6 * D), 0.5 * D ** -0.5),
        "b_ada": nrm(ks[7], (DEPTH, 6 * D), 0.01),
        "g_pre1": 1.0 + nrm(ks[8], (DEPTH, D), 0.05),
        "g_post1": 1.0 + nrm(ks[9], (DEPTH, D), 0.05),
        "g_pre2": 1.0 + nrm(ks[10], (DEPTH, D), 0.05),
        "g_post2": 1.0 + nrm(ks[11], (DEPTH, D), 0.05),
        "w_in": nrm(ks[12], (DEPTH, D, IN_COLS), D ** -0.5),
        "hgrn_lb": nrm(ks[13], (DEPTH + 1, D_A), 1.0),
        "hgrn_onorm": 1.0 + nrm(ks[14], (DEPTH, H_A, HEAD_V), 0.05),
        "conv_w": nrm(ks[15], (DEPTH, CONV_W, D_B), CONV_W ** -0.5),
        "p_a": nrm(ks[16], (DEPTH, D_A, D), D_A ** -0.5),
        "p_b": nrm(ks[17], (DEPTH, D_B, D), D_B ** -0.5),
        "w_o": nrm(ks[18], (DEPTH, D, D), D ** -0.5),
        "peer_wq": nrm(ks[19], (DEPTH, D, H_P * D_QUERY), D ** -0.5),
        "peer_subkeys": nrm(ks[20], (DEPTH, H_P, 2, N_KEYS, D_HALF), D_HALF ** -0.5),
        "peer_u": nrm(ks[21], (DEPTH, N_EXP, D), D ** -0.5),
        "peer_v": nrm(ks[22], (DEPTH, N_EXP, D), H_P ** -0.5),
    }


def reference(x_prompt, x_sample, state_hgrn, state_conv, c_prompt, c_sample, w_ada, b_ada,
              g_pre1, g_post1, g_pre2, g_post2, w_in, hgrn_lb, hgrn_onorm, conv_w, p_a, p_b, w_o,
              peer_wq, peer_subkeys, peer_u, peer_v):
    lbs = jnp.cumsum(jax.nn.softmax(hgrn_lb.astype(jnp.float32), axis=0), axis=0)
    bsz = x_prompt.shape[0]
    sp0 = jnp.zeros((bsz, H_A, HEAD_K, HEAD_V), jnp.float32)
    bp0 = jnp.zeros((bsz, CONV_W - 1, D_B), x_prompt.dtype)
    xp, xs = x_prompt, x_sample
    hp, cp, hs, cs = [], [], [], []
    for l in range(DEPTH):
        lb = lbs[l].reshape(H_A, HEAD_K)
        w = (w_ada[l], b_ada[l], g_pre1[l], g_post1[l], g_pre2[l], g_post2[l], w_in[l], hgrn_onorm[l],
             conv_w[l], p_a[l], p_b[l], w_o[l], peer_wq[l], peer_subkeys[l], peer_u[l], peer_v[l])
        xp, sp, bp = hybrid_layer(xp, c_prompt, sp0, bp0, lb, True, *w)
        xs, ss, bs = hybrid_layer(xs, c_sample, state_hgrn[l], state_conv[l], lb, False, *w)
        hp.append(sp)
        cp.append(bp)
        hs.append(ss)
        cs.append(bs)
    return (xp, xs, jnp.stack(hp), jnp.stack(cp), jnp.stack(hs), jnp.stack(cs))
```

```python
import functools

import numpy as np
import jax
import jax.numpy as jnp
from jax import lax
from jax.experimental import pallas as pl
from jax.experimental.pallas import tpu as pltpu

EPS = 1e-6
F32 = jnp.float32
BF16 = jnp.bfloat16

LANES = 128
SUBLANES = 8
V7X_VMEM_LIMIT_CAP = 60000 * 1024

HEAD = 128
ROWS = 128
TOPK = 16
HGRN_CHUNK = 128
HGRN_DIAG = 8
N_MOD = 6


def _cparams(semantics, vmem_bytes):
    return pltpu.CompilerParams(dimension_semantics=semantics,
                                vmem_limit_bytes=int(min(vmem_bytes, V7X_VMEM_LIMIT_CAP)))


def _dot(a, b):
    return jnp.dot(a, b, preferred_element_type=F32)


def _dot_nt(a, b):
    return lax.dot_general(a, b, (((1,), (1,)), ((), ())), preferred_element_type=F32)


def _dot_tn(a, b):
    return lax.dot_general(a, b, (((0,), (0,)), ((), ())), preferred_element_type=F32)


def _rms(x, g):
    return x * lax.rsqrt(jnp.mean(x * x, axis=-1, keepdims=True) + EPS) * g


def _silu(x):
    return x * jax.nn.sigmoid(x)


def _pick(n, cands):
    for c in cands:
        if n % c == 0:
            return c
    raise ValueError(f"no tile in {cands} divides {n}")


def _mod_kernel(cp_ref, cs_ref, w_ref, b_ref, op_ref, os_ref):
    w = w_ref[...].astype(BF16)
    b = b_ref[...]
    op_ref[...] = _dot(_silu(cp_ref[...]).astype(BF16), w) + b
    os_ref[...] = _dot(_silu(cs_ref[...]).astype(BF16), w) + b


def _modulation(c_p, c_s, w_ada, b_ada):
    d, n = w_ada.shape
    tn = _pick(n, (512, 256, 128))
    rp, rs = c_p.shape[0], c_s.shape[0]
    return pl.pallas_call(
        _mod_kernel,
        grid=(n // tn,),
        in_specs=[pl.BlockSpec((rp, d), lambda j: (0, 0)),
                  pl.BlockSpec((rs, d), lambda j: (0, 0)),
                  pl.BlockSpec((d, tn), lambda j: (0, j)),
                  pl.BlockSpec((1, tn), lambda j: (0, j))],
        out_specs=[pl.BlockSpec((rp, tn), lambda j: (0, j)),
                   pl.BlockSpec((rs, tn), lambda j: (0, j))],
        out_shape=[jax.ShapeDtypeStruct((rp, n), F32), jax.ShapeDtypeStruct((rs, n), F32)],
        compiler_params=_cparams(("parallel",), 2 * (d * tn * 4 + (rp + rs) * (d + tn) * 4) + d * tn * 2 + (8 << 20)),
        name="adaln_mod",
    )(c_p, c_s, w_ada, b_ada)


class _Rows:
    def __init__(self, n_prompt_rows, seq, d):
        self.np_tiles = n_prompt_rows // ROWS
        self.tiles_per_batch = seq // ROWS
        self.n_tiles = self.np_tiles + 1
        self.d = d

    def prompt_rows(self, width):
        last = self.np_tiles - 1
        return pl.BlockSpec((ROWS, width), lambda i: (jnp.minimum(i, last), 0))

    def sample_rows(self, width):
        return pl.BlockSpec((ROWS, width), lambda i: (0, 0))

    def all_rows(self, width):
        return pl.BlockSpec((ROWS, width), lambda i: (i, 0))

    def mod_prompt(self, k):
        last = self.np_tiles - 1
        tpb = self.tiles_per_batch
        return pl.BlockSpec((1, 1, self.d), lambda i: (jnp.minimum(i, last) // tpb, 0, k))

    def mod_sample(self, k):
        return pl.BlockSpec((ROWS, self.d), lambda i: (0, k))

    def vec(self):
        return pl.BlockSpec((1, self.d), lambda i: (0, 0))


def _pre1_kernel(xp_ref, xs_ref, g_ref, scp_ref, shp_ref, scs_ref, shs_ref, u_ref, *, np_tiles):
    i = pl.program_id(0)

    @pl.when(i < np_tiles)
    def _():
        u_ref[...] = (_rms(xp_ref[...], g_ref[...]) * (1 + scp_ref[0]) + shp_ref[0]).astype(u_ref.dtype)

    @pl.when(i >= np_tiles)
    def _():
        u_ref[...] = (_rms(xs_ref[...], g_ref[...]) * (1 + scs_ref[...]) + shs_ref[...]).astype(u_ref.dtype)


def _pre1(rows, x_p, x_s, g, mod_p3, mod_s):
    d = rows.d
    n_rows = x_p.shape[0] + ROWS
    return pl.pallas_call(
        functools.partial(_pre1_kernel, np_tiles=rows.np_tiles),
        grid=(rows.n_tiles,),
        in_specs=[rows.prompt_rows(d), rows.sample_rows(d), rows.vec(),
                  rows.mod_prompt(1), rows.mod_prompt(0), rows.mod_sample(1), rows.mod_sample(0)],
        out_specs=rows.all_rows(d),
        out_shape=jax.ShapeDtypeStruct((n_rows, d), BF16),
        compiler_params=_cparams(("arbitrary",), 16 * ROWS * d * 4 + (4 << 20)),
        name="pre_norm1",
    )(x_p, x_s, g, mod_p3, mod_p3, mod_s, mod_s)


def _mm_kernel(a_ref, b_ref, o_ref, *, nk):
    acc = _dot(a_ref[...], b_ref[...])
    if nk == 1:
        o_ref[...] = acc.astype(o_ref.dtype)
    else:
        k = pl.program_id(2)

        @pl.when(k == 0)
        def _():
            o_ref[...] = acc

        @pl.when(k > 0)
        def _():
            o_ref[...] += acc


def _matmul(a, b, out_dtype, name, tk=None):
    m, kd = a.shape
    n = b.shape[1]
    tm = _pick(m, (640, 512, 256, 128))
    tn = _pick(n, (1024, 512, 256, 128))
    tk = kd if tk is None else tk
    nk = kd // tk
    assert nk == 1 or out_dtype == F32
    osz = jnp.dtype(out_dtype).itemsize
    vmem = 2 * (tm * tk * 2 + tk * tn * 2 + tm * tn * osz) + 2 * tm * tn * 4 + (4 << 20)
    return pl.pallas_call(
        functools.partial(_mm_kernel, nk=nk),
        grid=(m // tm, n // tn, nk),
        in_specs=[pl.BlockSpec((tm, tk), lambda i, j, k: (i, k)),
                  pl.BlockSpec((tk, tn), lambda i, j, k: (k, j))],
        out_specs=pl.BlockSpec((tm, tn), lambda i, j, k: (i, j)),
        out_shape=jax.ShapeDtypeStruct((m, n), out_dtype),
        compiler_params=_cparams(("parallel", "parallel", "arbitrary"), vmem),
        name=name,
    )(a, b)


def _hgrn_level_widths(chunk, diag):
    widths, w = [], chunk // 2
    while w >= diag:
        widths.append(w)
        w //= 2
    return widths


def _hgrn_sum_matrices(chunk, diag):
    t = np.arange(chunk)[:, None]
    r = np.arange(chunk)[None, :]
    mats = []
    for w in _hgrn_level_widths(chunk, diag):
        mid = (t // (2 * w)) * 2 * w + w - 1
        upper = (t % (2 * w)) >= w
        mats.append(np.where(upper, (r > mid) & (r <= t), (r > t) & (r <= mid)))
    mats.append((r >= (t // diag) * diag) & (r <= t))
    mats.append(r <= t)
    mats.append(r > t)
    return np.concatenate(mats, axis=0).astype(np.float32)


def _layer_lower_bound(lb_raw, layer):
    e = jnp.exp(lb_raw - jnp.max(lb_raw, axis=0, keepdims=True))
    return jnp.sum(e[: layer + 1], axis=0, keepdims=True) / jnp.sum(e, axis=0, keepdims=True)


def _log2(n):
    assert n > 0 and n & (n - 1) == 0
    return n.bit_length() - 1


def _hgrn_p_kernel(q_ref, fz_ref, iv_ref, og_ref, lb_ref, on_ref, sm_ref, a_ref, st_ref, sT_ref,
                   *, layer, heads, chunk, diag):
    c = pl.program_id(2)

    @pl.when(c == 0)
    def _():
        sT_ref[...] = jnp.zeros_like(sT_ref)

    widths = _hgrn_level_widths(chunk, diag)
    nlev = len(widths)
    row = lax.broadcasted_iota(jnp.int32, (chunk, HEAD), 0)
    ti = lax.broadcasted_iota(jnp.int32, (chunk, chunk), 0)
    si = lax.broadcasted_iota(jnp.int32, (chunk, chunk), 1)
    upper = [(row & w) != 0 for w in widths]
    same_block = [(ti >> _log2(2 * w)) == (si >> _log2(2 * w)) for w in widths]
    diag_mask = ((ti >> _log2(diag)) == (si >> _log2(diag))) & (si <= ti)
    lb_all = _layer_lower_bound(lb_ref[...], layer)
    sm = sm_ref[...]

    for h in range(heads):
        sl = slice(h * HEAD, (h + 1) * HEAD)
        q = q_ref[:, sl]
        v = iv_ref[:, sl].astype(BF16)
        lb = lb_all[:, sl]
        f = lb + (1 - lb) * jax.nn.sigmoid(fz_ref[:, sl])
        g = jnp.log(f)
        k = 1 - f
        g1 = g.astype(BF16)
        r1 = g - g1.astype(F32)
        g2 = r1.astype(BF16)
        g3 = (r1 - g2.astype(F32)).astype(BF16)
        e3 = _dot(sm, jnp.concatenate([g1, g2, g3], axis=1))
        e = e3[:, :HEAD] + e3[:, HEAD:2 * HEAD] + e3[:, 2 * HEAD:]

        att = jnp.zeros((chunk, chunk), F32)
        for l in range(nlev):
            fac = jnp.exp(e[l * chunk:(l + 1) * chunk])
            qt = (q * jnp.where(upper[l], fac, 0.0)).astype(BF16)
            kt = (k * jnp.where(upper[l], 0.0, fac)).astype(BF16)
            att = att + jnp.where(same_block[l], _dot_nt(qt, kt), 0.0)
        ed = e[nlev * chunk:(nlev + 1) * chunk]
        att = att + jnp.where(diag_mask,
                              _dot_nt((q * jnp.exp(ed)).astype(BF16), (k * jnp.exp(-ed)).astype(BF16)), 0.0)
        ecum = e[(nlev + 1) * chunk:(nlev + 2) * chunk]
        erev = e[(nlev + 2) * chunk:(nlev + 3) * chunk]
        sT = sT_ref[h]
        o = _dot(att.astype(BF16), v) + _dot_nt((q * jnp.exp(ecum)).astype(BF16), sT.astype(BF16))
        sT_new = jnp.exp(ecum[chunk - 1:chunk, :]) * sT + _dot_tn(v, (k * jnp.exp(erev)).astype(BF16))
        sT_ref[h] = sT_new
        a_ref[:, sl] = (_rms(o, on_ref[:, sl]) * _silu(og_ref[:, sl])).astype(a_ref.dtype)

        @pl.when(c == pl.num_programs(2) - 1)
        def _():
            st_ref[0, h] = sT_new.T


def _hgrn_prompt(proj, lb_raw, onorm_row, batch, seq, d_a, n_rows, layer):
    heads = _pick(d_a // HEAD, (4, 2, 1))
    width = heads * HEAD
    chunk = HGRN_CHUNK
    cps = seq // chunk
    nblk = d_a // width
    sm = jnp.asarray(_hgrn_sum_matrices(chunk, HGRN_DIAG), BF16)

    def col(k):
        return pl.BlockSpec((chunk, width), lambda b, g, c: (b * cps + c, k * nblk + g))

    return pl.pallas_call(
        functools.partial(_hgrn_p_kernel, layer=layer, heads=heads, chunk=chunk, diag=HGRN_DIAG),
        grid=(batch, nblk, cps),
        in_specs=[col(0), col(1), col(2), col(3),
                  pl.BlockSpec((lb_raw.shape[0], width), lambda b, g, c: (0, g)),
                  pl.BlockSpec((1, width), lambda b, g, c: (0, g)),
                  pl.BlockSpec(sm.shape, lambda b, g, c: (0, 0))],
        out_specs=[pl.BlockSpec((chunk, width), lambda b, g, c: (b * cps + c, g)),
                   pl.BlockSpec((1, heads, HEAD, HEAD), lambda b, g, c: (b, g, 0, 0))],
        out_shape=[jax.ShapeDtypeStruct((n_rows, d_a), BF16),
                   jax.ShapeDtypeStruct((batch, d_a // HEAD, HEAD, HEAD), F32)],
        scratch_shapes=[pltpu.VMEM((heads, HEAD, HEAD), F32)],
        compiler_params=_cparams(("parallel", "parallel", "arbitrary"), 32 << 20),
        name="hgrn_prompt",
    )(proj, proj, proj, proj, lb_raw, onorm_row, sm)


def _hgrn_s_kernel(q_ref, fz_ref, iv_ref, og_ref, lb_ref, on_ref, s_ref, a_in_ref, a_ref, so_ref, o_scr,
                   *, layer, heads):
    del a_in_ref
    eye = (lax.broadcasted_iota(jnp.int32, (HEAD, HEAD), 0)
           == lax.broadcasted_iota(jnp.int32, (HEAD, HEAD), 1)).astype(F32)

    def column(r):
        return jnp.sum(eye * r, axis=1, keepdims=True)

    lb_all = _layer_lower_bound(lb_ref[...], layer)[0]
    f_all = lb_all + (1 - lb_all) * jax.nn.sigmoid(fz_ref[0])
    for h in range(heads):
        f = f_all[h:h + 1, :]
        s_new = column(f) * s_ref[0, h] + column(1 - f) * iv_ref[0, h:h + 1, :]
        so_ref[0, h] = s_new
        o_scr[h:h + 1, :] = jnp.sum(column(q_ref[0, h:h + 1, :]) * s_new, axis=0, keepdims=True)
    a_ref[0] = (_rms(o_scr[...], on_ref[...]) * _silu(og_ref[0])).astype(a_ref.dtype)


def _hgrn_sample(proj, lb_raw, onorm, state, a_all, n_prompt_rows, d_a, layer):
    n_rows = proj.shape[0]
    heads = d_a // HEAD
    nb = state.shape[0]
    proj4 = proj.reshape(n_rows, proj.shape[1] // d_a, heads, HEAD)
    a3 = a_all.reshape(n_rows, heads, HEAD)
    lb3 = lb_raw.reshape(lb_raw.shape[0], heads, HEAD)

    def col(k):
        return pl.BlockSpec((1, None, heads, HEAD), lambda b: (n_prompt_rows + b, k, 0, 0))

    a_new, s_new = pl.pallas_call(
        functools.partial(_hgrn_s_kernel, layer=layer, heads=heads),
        grid=(nb,),
        in_specs=[col(0), col(1), col(2), col(3),
                  pl.BlockSpec(lb3.shape, lambda b: (0, 0, 0)),
                  pl.BlockSpec((heads, HEAD), lambda b: (0, 0)),
                  pl.BlockSpec((1, heads, HEAD, HEAD), lambda b: (b, 0, 0, 0)),
                  pl.BlockSpec(memory_space=pl.ANY)],
        out_specs=[pl.BlockSpec((1, heads, HEAD), lambda b: (n_prompt_rows + b, 0, 0)),
                   pl.BlockSpec((1, heads, HEAD, HEAD), lambda b: (b, 0, 0, 0))],
        out_shape=[jax.ShapeDtypeStruct(a3.shape, a3.dtype), jax.ShapeDtypeStruct(state.shape, F32)],
        input_output_aliases={7: 0},
        scratch_shapes=[pltpu.VMEM((heads, HEAD), F32)],
        compiler_params=_cparams(("arbitrary",), 4 * heads * HEAD * HEAD * 4 + (8 << 20)),
        name="hgrn_sample",
    )(proj4, proj4, proj4, proj4, lb3, onorm, state, a3)
    return a_new.reshape(n_rows, d_a), s_new


def _conv_p_kernel(hb_ref, bg_ref, cg_ref, w_ref, z_ref, buf_ref):
    u = cg_ref[...] * hb_ref[...]
    n = u.shape[0]
    row = lax.broadcasted_iota(jnp.int32, u.shape, 0)
    u1 = jnp.where(row >= 1, pltpu.roll(u, 1, 0), 0.0)
    u2 = jnp.where(row >= 2, pltpu.roll(u, 2, 0), 0.0)
    y = (u2 * w_ref[0:1, :] + u1 * w_ref[1:2, :]) + u * w_ref[2:3, :]
    z_ref[...] = (bg_ref[...] * y).astype(z_ref.dtype)
    buf_ref[0] = u[n - 2:n, :]


def _conv_prompt(proj, conv_w, batch, seq, d_a, d_b, n_rows):
    cb = _pick(d_b, (256, 128))
    base = 4 * d_a // cb
    nb = d_b // cb

    def col(k):
        return pl.BlockSpec((seq, cb), lambda b, j: (b, base + k * nb + j))

    return pl.pallas_call(
        _conv_p_kernel,
        grid=(batch, nb),
        in_specs=[col(0), col(1), col(2), pl.BlockSpec((conv_w.shape[0], cb), lambda b, j: (0, j))],
        out_specs=[pl.BlockSpec((seq, cb), lambda b, j: (b, j)),
                   pl.BlockSpec((1, 2, cb), lambda b, j: (b, 0, j))],
        out_shape=[jax.ShapeDtypeStruct((n_rows, d_b), BF16), jax.ShapeDtypeStruct((batch, 2, d_b), F32)],
        compiler_params=_cparams(("parallel", "parallel"), 16 * seq * cb * 4 + (4 << 20)),
        name="conv_prompt",
    )(proj, proj, proj, conv_w)


def _conv_s_kernel(hb_ref, bg_ref, cg_ref, b0_ref, b1_ref, w_ref, z_in_ref, z_ref, u_ref):
    del z_in_ref
    u = cg_ref[...] * hb_ref[...]
    y = (b0_ref[...] * w_ref[0:1, :] + b1_ref[...] * w_ref[1:2, :]) + u * w_ref[2:3, :]
    z_ref[...] = (bg_ref[...] * y).astype(z_ref.dtype)
    u_ref[...] = u


def _conv_sample(proj, conv_w, buf0, buf1, z_all, n_prompt_rows, d_a, d_b):
    cb = _pick(d_b, (256, 128))
    base = 4 * d_a // cb
    nb = d_b // cb
    rt = n_prompt_rows // ROWS

    def col(k):
        return pl.BlockSpec((ROWS, cb), lambda j: (rt, base + k * nb + j))

    vec = pl.BlockSpec((ROWS, cb), lambda j: (0, j))
    return pl.pallas_call(
        _conv_s_kernel,
        grid=(nb,),
        in_specs=[col(0), col(1), col(2), vec, vec,
                  pl.BlockSpec((conv_w.shape[0], cb), lambda j: (0, j)),
                  pl.BlockSpec(memory_space=pl.ANY)],
        out_specs=[pl.BlockSpec((ROWS, cb), lambda j: (rt, j)), vec],
        out_shape=[jax.ShapeDtypeStruct(z_all.shape, z_all.dtype), jax.ShapeDtypeStruct((ROWS, d_b), F32)],
        input_output_aliases={6: 0},
        compiler_params=_cparams(("arbitrary",), 16 * ROWS * cb * 4 + (4 << 20)),
        name="conv_sample",
    )(proj, proj, proj, buf0, buf1, conv_w, z_all)


def _merge_kernel(a_ref, z_ref, pa_ref, pb_ref, ga_ref, gb_ref, o_ref):
    ya = _dot(a_ref[...], pa_ref[...])
    yb = _dot(z_ref[...], pb_ref[...])
    o_ref[...] = (jax.nn.sigmoid(ga_ref[...]) * ya + jax.nn.sigmoid(gb_ref[...]) * yb).astype(o_ref.dtype)


def _merge(a_all, z_all, p_a, p_b, proj, d_a, d_b, d):
    m = a_all.shape[0]
    tm = _pick(m, (640, 512, 256, 128))
    tn = _pick(np.gcd(d, 4 * d_a + 3 * d_b), (1024, 512, 256, 128))
    gbase = (4 * d_a + 3 * d_b) // tn
    nj = d // tn
    vmem = 2 * (tm * (d_a + d_b) * 2 + (d_a + d_b) * tn * 2 + 2 * tm * tn * 4 + tm * tn * 2) + 4 * tm * tn * 4 + (4 << 20)
    return pl.pallas_call(
        _merge_kernel,
        grid=(m // tm, nj),
        in_specs=[pl.BlockSpec((tm, d_a), lambda i, j: (i, 0)),
                  pl.BlockSpec((tm, d_b), lambda i, j: (i, 0)),
                  pl.BlockSpec((d_a, tn), lambda i, j: (0, j)),
                  pl.BlockSpec((d_b, tn), lambda i, j: (0, j)),
                  pl.BlockSpec((tm, tn), lambda i, j: (i, gbase + j)),
                  pl.BlockSpec((tm, tn), lambda i, j: (i, gbase + nj + j))],
        out_specs=pl.BlockSpec((tm, tn), lambda i, j: (i, j)),
        out_shape=jax.ShapeDtypeStruct((m, d), BF16),
        compiler_params=_cparams(("parallel", "parallel"), vmem),
        name="merge_branches",
    )(a_all, z_all, p_a, p_b, proj, proj)


def _mid_kernel(m_ref, xp_ref, xs_ref, gpost_ref, gpre_ref, gtp_ref, scp_ref, shp_ref,
                gts_ref, scs_ref, shs_ref, x1_ref, u2_ref, *, np_tiles):
    i = pl.program_id(0)

    def body(x, gt, sc, sh):
        x1 = x + gt * _rms(m_ref[...], gpost_ref[...])
        x1_ref[...] = x1
        u2_ref[...] = (_rms(x1, gpre_ref[...]) * (1 + sc) + sh).astype(u2_ref.dtype)

    @pl.when(i < np_tiles)
    def _():
        body(xp_ref[...], gtp_ref[0], scp_ref[0], shp_ref[0])

    @pl.when(i >= np_tiles)
    def _():
        body(xs_ref[...], gts_ref[...], scs_ref[...], shs_ref[...])


def _mid(rows, m_out, x_p, x_s, g_post1, g_pre2, mod_p3, mod_s):
    d = rows.d
    n_rows = m_out.shape[0]
    return pl.pallas_call(
        functools.partial(_mid_kernel, np_tiles=rows.np_tiles),
        grid=(rows.n_tiles,),
        in_specs=[rows.all_rows(d), rows.prompt_rows(d), rows.sample_rows(d), rows.vec(), rows.vec(),
                  rows.mod_prompt(2), rows.mod_prompt(4), rows.mod_prompt(3),
                  rows.mod_sample(2), rows.mod_sample(4), rows.mod_sample(3)],
        out_specs=[rows.all_rows(d), rows.all_rows(d)],
        out_shape=[jax.ShapeDtypeStruct((n_rows, d), F32), jax.ShapeDtypeStruct((n_rows, d), BF16)],
        compiler_params=_cparams(("arbitrary",), 24 * ROWS * d * 4 + (4 << 20)),
        name="residual1_pre_norm2",
    )(m_out, x_p, x_s, g_post1, g_pre2, mod_p3, mod_p3, mod_p3, mod_s, mod_s, mod_s)


def _score_kernel(u_ref, wq_ref, key_ref, s1_ref, s2_ref):
    q = _dot(u_ref[...], wq_ref[...]).astype(BF16)
    s1_ref[0] = _dot_nt(key_ref[0, 0], q[:, :HEAD])
    s2_ref[0] = _dot_nt(key_ref[0, 1], q[:, HEAD:])


def _peer_scores(u2, wq, subkeys):
    m, d = u2.shape
    hp = subkeys.shape[0]
    tm = _pick(m, (640, 512, 256, 128))
    shp = jax.ShapeDtypeStruct((hp, HEAD, m), F32)
    out = pl.BlockSpec((1, HEAD, tm), lambda i, h: (h, 0, i))
    return pl.pallas_call(
        _score_kernel,
        grid=(m // tm, hp),
        in_specs=[pl.BlockSpec((tm, d), lambda i, h: (i, 0)),
                  pl.BlockSpec((d, 2 * HEAD), lambda i, h: (0, h)),
                  pl.BlockSpec((1, 2, HEAD, HEAD), lambda i, h: (h, 0, 0, 0))],
        out_specs=[out, out],
        out_shape=[shp, shp],
        compiler_params=_cparams(("parallel", "arbitrary"), 2 * (tm * d * 2 + d * 2 * HEAD * 2) + (8 << 20)),
        name="peer_scores",
    )(u2, wq, subkeys)


def _top_values(x, n):
    vals = []
    for _ in range(n):
        m = jnp.max(x, axis=0, keepdims=True)
        vals.append(m)
        x = jnp.where(x == m, -jnp.inf, x)
    return vals


def _candidate_pairs():
    return [(a, b) for a in range(TOPK) for b in range(TOPK) if (a + 1) * (b + 1) <= TOPK]


def _gate_kernel(s1_ref, s2_ref, p1_ref, e2_ref, tau_ref, c_scr):
    s1 = s1_ref[0]
    s2 = s2_ref[0]
    v1 = _top_values(s1, TOPK)
    v2 = _top_values(s2, TOPK)
    c_scr[...] = jnp.full(c_scr.shape, -jnp.inf, F32)
    for r, (a, b) in enumerate(_candidate_pairs()):
        c_scr[r:r + 1, :] = v1[a] + v2[b]
    c = c_scr[...]
    tau = _top_values(c, TOPK)[-1]
    top = v1[0] + v2[0]
    z = jnp.sum(jnp.where(c >= tau, jnp.exp(c - top), 0.0), axis=0, keepdims=True)
    p1_ref[0] = jnp.exp(s1 - v1[0]) / z
    e2_ref[0] = jnp.exp(s2 - v2[0])
    tau_ref[0] = jnp.broadcast_to(tau, tau_ref.shape[1:])


def _peer_gates(s1t, s2t):
    hp, _, m = s1t.shape
    tl = _pick(m, (640, 512, 256, 128))
    blk = pl.BlockSpec((1, HEAD, tl), lambda h, i: (h, 0, i))
    n_cand = -(-len(_candidate_pairs()) // SUBLANES) * SUBLANES
    return pl.pallas_call(
        _gate_kernel,
        grid=(hp, m // tl),
        in_specs=[blk, blk],
        out_specs=[blk, blk, pl.BlockSpec((1, SUBLANES, tl), lambda h, i: (h, 0, i))],
        out_shape=[jax.ShapeDtypeStruct(s1t.shape, F32), jax.ShapeDtypeStruct(s1t.shape, F32),
                   jax.ShapeDtypeStruct((hp, SUBLANES, m), F32)],
        scratch_shapes=[pltpu.VMEM((n_cand, tl), F32)],
        compiler_params=_cparams(("parallel", "parallel"), 32 << 20),
        name="peer_gates",
    )(s1t, s2t)


def _hidden_kernel(u_ref, ut_ref, s1_ref, p1_ref, s2_ref, e2_ref, tau_ref, h_ref, *, heads, groups):
    act = _dot_nt(ut_ref[...], u_ref[...])
    parts = []
    for gi in range(groups):
        gate = jnp.zeros((HEAD, act.shape[1]), F32)
        for h in range(heads):
            sel = (s1_ref[h, gi:gi + 1, :] + s2_ref[h]) >= tau_ref[h, 0:1, :]
            gate = gate + jnp.where(sel, p1_ref[h, gi:gi + 1, :], 0.0) * e2_ref[h]
        parts.append(gate * jax.nn.gelu(act[gi * HEAD:(gi + 1) * HEAD]))
    h_ref[...] = jnp.concatenate(parts, axis=0).T.astype(h_ref.dtype)


def _peer_hidden(u2, u_tab, s1t, p1t, s2t, e2t, tau):
    m, d = u2.shape
    n_exp = u_tab.shape[0]
    hp = s1t.shape[0]
    tm = _pick(m, (640, 512, 256, 128))
    te = _pick(n_exp, (1024,))
    groups = te // HEAD
    rowsel = pl.BlockSpec((hp, groups, tm), lambda i, j: (0, j, i))
    full = pl.BlockSpec((hp, HEAD, tm), lambda i, j: (0, 0, i))
    vmem = (2 * (tm * d * 2 + te * d * 2 + 2 * hp * (groups + HEAD) * tm * 4 + hp * SUBLANES * tm * 4 + tm * te * 2)
            + 4 * te * tm * 4 + (4 << 20))
    return pl.pallas_call(
        functools.partial(_hidden_kernel, heads=hp, groups=groups),
        grid=(m // tm, n_exp // te),
        in_specs=[pl.BlockSpec((tm, d), lambda i, j: (i, 0)),
                  pl.BlockSpec((te, d), lambda i, j: (j, 0)),
                  rowsel, rowsel, full, full,
                  pl.BlockSpec((hp, SUBLANES, tm), lambda i, j: (0, 0, i))],
        out_specs=pl.BlockSpec((tm, te), lambda i, j: (i, j)),
        out_shape=jax.ShapeDtypeStruct((m, n_exp), BF16),
        compiler_params=_cparams(("parallel", "arbitrary"), vmem),
        name="peer_hidden",
    )(u2, u_tab, s1t, p1t, s2t, e2t, tau)


def _final_kernel(y_ref, x1_ref, g_ref, gtp_ref, gts_ref, op_ref, os_ref, *, np_tiles):
    i = pl.program_id(0)
    r = _rms(y_ref[...], g_ref[...])

    @pl.when(i < np_tiles)
    def _():
        op_ref[...] = x1_ref[...] + gtp_ref[0] * r

    @pl.when(i >= np_tiles)
    def _():
        os_ref[...] = x1_ref[...] + gts_ref[...] * r


def _final(rows, y2, x1, g_post2, mod_p3, mod_s):
    d = rows.d
    n_p = rows.np_tiles * ROWS
    return pl.pallas_call(
        functools.partial(_final_kernel, np_tiles=rows.np_tiles),
        grid=(rows.n_tiles,),
        in_specs=[rows.all_rows(d), rows.all_rows(d), rows.vec(), rows.mod_prompt(5), rows.mod_sample(5)],
        out_specs=[rows.prompt_rows(d), rows.sample_rows(d)],
        out_shape=[jax.ShapeDtypeStruct((n_p, d), F32), jax.ShapeDtypeStruct((ROWS, d), F32)],
        compiler_params=_cparams(("arbitrary",), 16 * ROWS * d * 4 + (4 << 20)),
        name="residual2",
    )(y2, x1, g_post2, mod_p3, mod_s)


def _layer(layer, x_p, x_s, state_hgrn, state_conv, c_p16, c_s, batch, seq, w_ada, b_ada, g_pre1, g_post1,
           g_pre2, g_post2, w_in, hgrn_lb, onorm, conv_w, p_a, p_b, w_o, wq, subkeys, u_tab, v_tab):
    d = x_p.shape[1]
    d_a = onorm.shape[0] * HEAD
    d_b = conv_w.shape[1]
    n_p = x_p.shape[0]
    n_rows = n_p + ROWS
    rows = _Rows(n_p, seq, d)

    mod_p, mod_s = _modulation(c_p16, c_s, w_ada, b_ada.reshape(1, -1))
    mod_p3 = mod_p.reshape(mod_p.shape[0], 1, mod_p.shape[1])
    vec = lambda g: g.reshape(1, d)

    u = _pre1(rows, x_p, x_s, vec(g_pre1), mod_p3, mod_s)
    proj = _matmul(u, w_in.astype(BF16), F32, "in_proj")

    a_all, hgrn_p = _hgrn_prompt(proj, hgrn_lb, onorm.reshape(1, d_a), batch, seq, d_a, n_rows, layer)
    a_all, hgrn_s = _hgrn_sample(proj, hgrn_lb, onorm, state_hgrn, a_all, n_p, d_a, layer)
    z_all, conv_p = _conv_prompt(proj, conv_w, batch, seq, d_a, d_b, n_rows)
    z_all, u_s = _conv_sample(proj, conv_w, state_conv[:, 0, :], state_conv[:, 1, :], z_all, n_p, d_a, d_b)
    conv_s = jnp.stack([state_conv[:, 1, :], u_s], axis=1)

    merged = _merge(a_all, z_all, p_a.astype(BF16), p_b.astype(BF16), proj, d_a, d_b, d)
    m_out = _matmul(merged, w_o.astype(BF16), F32, "out_proj")
    x1, u2 = _mid(rows, m_out, x_p, x_s, vec(g_post1), vec(g_pre2), mod_p3, mod_s)

    s1t, s2t = _peer_scores(u2, wq.astype(BF16), subkeys.astype(BF16))
    p1t, e2t, tau = _peer_gates(s1t, s2t)
    hid = _peer_hidden(u2, u_tab.astype(BF16), s1t, p1t, s2t, e2t, tau)
    y2 = _matmul(hid, v_tab.astype(BF16), F32, "peer_out", tk=_pick(hid.shape[1], (2048,)))
    y_p, y_s = _final(rows, y2, x1, vec(g_post2), mod_p3, mod_s)
    return y_p, y_s, hgrn_p, conv_p, hgrn_s, conv_s


def kernel(x_prompt, x_sample, state_hgrn, state_conv, c_prompt, c_sample, w_ada, b_ada, g_pre1, g_post1,
           g_pre2, g_post2, w_in, hgrn_lb, hgrn_onorm, conv_w, p_a, p_b, w_o, peer_wq, peer_subkeys,
           peer_u, peer_v):
    batch, seq, d = x_prompt.shape
    nb, dec_seq, _ = x_sample.shape
    depth = w_ada.shape[0]
    assert dec_seq == 1 and nb == ROWS and seq % HGRN_CHUNK == 0 and batch <= 16
    assert hgrn_onorm.shape[2] == HEAD and peer_subkeys.shape[3] == HEAD and peer_subkeys.shape[4] == HEAD
    x_p = x_prompt.reshape(batch * seq, d)
    x_s = x_sample.reshape(nb, d)
    c_p16 = jnp.pad(c_prompt, ((0, 16 - batch), (0, 0)))
    hp, cp, hs, cs = [], [], [], []
    for l in range(depth):
        x_p, x_s, h_p, c_p, h_s, c_s = _layer(
            l, x_p, x_s, state_hgrn[l], state_conv[l], c_p16, c_sample, batch, seq, w_ada[l], b_ada[l],
            g_pre1[l], g_post1[l], g_pre2[l], g_post2[l], w_in[l], hgrn_lb, hgrn_onorm[l], conv_w[l],
            p_a[l], p_b[l], w_o[l], peer_wq[l], peer_subkeys[l], peer_u[l], peer_v[l])
        hp.append(h_p)
        cp.append(c_p)
        hs.append(h_s)
        cs.append(c_s)
    return (x_p.reshape(batch, seq, d), x_s.reshape(nb, 1, d), jnp.stack(hp), jnp.stack(cp),
            jnp.stack(hs), jnp.stack(cs))
```
